```python
import math
import jax, jax.numpy as jnp
from jax import lax
import numpy as np

D_MODEL = 2048
BATCH = 2
SEQ = 4096
DEPTH = 2

N_HEADS = 16
N_KV_HEADS = 4
HEAD_DIM = 64
ROT_DIM = HEAD_DIM // 4
ROPE_THETA = 500000.0
WINDOW = 128
BLOCK = 128
Q_WIDTH = N_HEADS * HEAD_DIM
KV_WIDTH = N_KV_HEADS * HEAD_DIM
D_CONV = D_MODEL // 2
CONV_K = 3
IN_COLS = Q_WIDTH + 2 * KV_WIDTH + 3 * D_CONV + 2 * D_MODEL
N_GROUPS = 4
EXPERTS_PER_GROUP = 8
N_EXPERTS = N_GROUPS * EXPERTS_PER_GROUP
TOP_K_IN_GROUP = 2
D_FF_EXPERT = 512
NORM_EPS = 1e-5

kernel_name = "hybrid_swa_shortconv_hiermoe"


def rmsnorm(x, g):
    xf = x.astype(jnp.float32)
    y = xf * lax.rsqrt(jnp.mean(xf * xf, axis=-1, keepdims=True) + NORM_EPS) * g.astype(jnp.float32)
    return y.astype(x.dtype)


def partial_rope(t, cos, sin):
    half = ROT_DIM // 2
    t1 = t[..., :half]
    t2 = t[..., half:ROT_DIM]
    c = cos[None, :, None, :]
    s = sin[None, :, None, :]
    return jnp.concatenate([t1 * c - t2 * s, t2 * c + t1 * s, t[..., ROT_DIM:]], axis=-1)


def sliding_window_attention(q, k, v, sinks):
    b, s, hq, dh = q.shape
    hkv = k.shape[2]
    grp = hq // hkv
    nb = s // BLOCK
    qb = q.reshape(b, nb, BLOCK, hkv, grp, dh)

    def with_prev(t):
        tb = t.reshape(b, nb, BLOCK, hkv, dh)
        prev = jnp.pad(tb, ((0, 0), (1, 0), (0, 0), (0, 0), (0, 0)))[:, :-1]
        return jnp.concatenate([prev, tb], axis=2)

    kk = with_prev(k)
    vv = with_prev(v)
    scores = jnp.einsum('bnqhgd,bnkhd->bnhgqk', qb, kk).astype(jnp.float32) * (1.0 / math.sqrt(dh))
    blk = jnp.arange(nb)[:, None] * BLOCK
    qpos = blk + jnp.arange(BLOCK)[None, :]
    kpos = blk - BLOCK + jnp.arange(2 * BLOCK)[None, :]
    diff = qpos[:, :, None] - kpos[:, None, :]
    mask = (diff >= 0) & (diff < WINDOW) & (kpos[:, None, :] >= 0)
    scores = jnp.where(mask[None, :, None, None], scores, -jnp.inf)
    sink_col = jnp.broadcast_to(sinks.astype(jnp.float32).reshape(1, 1, hkv, grp, 1, 1),
                                scores.shape[:-1] + (1,))
    probs = jax.nn.softmax(jnp.concatenate([scores, sink_col], axis=-1), axis=-1)[..., :-1]
    out = jnp.einsum('bnhgqk,bnkhd->bnqhgd', probs.astype(v.dtype), vv)
    return out.reshape(b, s, hq * dh)


def causal_short_conv(u, w):
    s = u.shape[1]
    kw = w.shape[0]
    up = jnp.pad(u, ((0, 0), (kw - 1, 0), (0, 0)))
    y = up[:, 0:s] * w[0]
    for j in range(1, kw):
        y = y + up[:, j:j + s] * w[j]
    return y


def hierarchical_moe(h, w_group, b_group, w_router, b_router, w_gate, w_up, w_down):
    b, s, d = h.shape
    n = b * s
    hf = h.reshape(n, d)
    g_logits = (hf @ w_group).astype(jnp.float32) + b_group.astype(jnp.float32)
    g_probs = jax.nn.softmax(g_logits, axis=-1)
    g_w, g_idx = lax.top_k(g_probs, 1)
    e_logits = ((hf @ w_router).astype(jnp.float32) + b_router.astype(jnp.float32)).reshape(n, N_GROUPS, EXPERTS_PER_GROUP)
    g_onehot = jax.nn.one_hot(g_idx[:, 0], N_GROUPS, dtype=jnp.float32)
    sel = jnp.einsum('ng,nge->ne', g_onehot, e_logits)
    e_probs = jax.nn.softmax(sel, axis=-1)
    e_w, e_idx = lax.top_k(e_probs, TOP_K_IN_GROUP)
    e_w = e_w / jnp.sum(e_w, axis=-1, keepdims=True)
    weights = g_w * e_w
    expert_id = g_idx * EXPERTS_PER_GROUP + e_idx
    combine = jnp.sum(jax.nn.one_hot(expert_id, N_EXPERTS, dtype=jnp.float32) * weights[..., None], axis=1)
    gate = jnp.einsum('nd,edf->enf', hf, w_gate)
    up = jnp.einsum('nd,edf->enf', hf, w_up)
    act = jax.nn.silu(gate) * up * combine.T.astype(hf.dtype)[:, :, None]
    out = jnp.einsum('enf,efd->nd', act, w_down)
    return out.reshape(b, s, d)


def setup_inputs(seed: int = 0) -> dict:
    key = jax.random.key(seed)
    ks = jax.random.split(key, 17)

    def nrm(k, shape, scale):
        return jax.random.normal(k, shape, jnp.float32) * scale

    return {
        "x": nrm(ks[0], (BATCH, SEQ, D_MODEL), 1.0),
        "g_mix": 1.0 + nrm(ks[1], (DEPTH, D_MODEL), 0.02),
        "w_in": nrm(ks[2], (DEPTH, D_MODEL, IN_COLS), D_MODEL ** -0.5),
        "sinks": nrm(ks[3], (DEPTH, N_HEADS), 0.5),
        "w_attn_out": nrm(ks[4], (DEPTH, Q_WIDTH, D_MODEL), Q_WIDTH ** -0.5),
        "conv_w": nrm(ks[5], (DEPTH, CONV_K, D_CONV), CONV_K ** -0.5),
        "w_conv_out": nrm(ks[6], (DEPTH, D_CONV, D_MODEL), D_CONV ** -0.5),
        "w_o": nrm(ks[7], (DEPTH, D_MODEL, D_MODEL), D_MODEL ** -0.5),
        "g_ffn": 1.0 + nrm(ks[8], (DEPTH, D_MODEL), 0.02),
        "w_group": nrm(ks[9], (DEPTH, D_MODEL, N_GROUPS), D_MODEL ** -0.5),
        "b_group": nrm(ks[10], (DEPTH, N_GROUPS), 0.01),
        "w_router": nrm(ks[11], (DEPTH, D_MODEL, N_EXPERTS), D_MODEL ** -0.5),
        "b_router": nrm(ks[12], (DEPTH, N_EXPERTS), 0.01),
        "w_gate": nrm(ks[13], (DEPTH, N_EXPERTS, D_MODEL, D_FF_EXPERT), D_MODEL ** -0.5),
        "w_up": nrm(ks[14], (DEPTH, N_EXPERTS, D_MODEL, D_FF_EXPERT), D_MODEL ** -0.5),
        "w_down": nrm(ks[15], (DEPTH, N_EXPERTS, D_FF_EXPERT, D_MODEL), D_FF_EXPERT ** -0.5),
        "g_final": 1.0 + nrm(ks[16], (D_MODEL,), 0.02),
    }


def reference(x, g_mix, w_in, sinks, w_attn_out, conv_w, w_conv_out, w_o, g_ffn,
              w_group, b_group, w_router, b_router, w_gate, w_up, w_down, g_final):
    b, s, d = x.shape
    pos = jnp.arange(s, dtype=jnp.float32)
    inv_freq = ROPE_THETA ** (-jnp.arange(0, ROT_DIM, 2, dtype=jnp.float32) / ROT_DIM)
    ang = pos[:, None] * inv_freq[None, :]
    cos = jnp.cos(ang).astype(x.dtype)
    sin = jnp.sin(ang).astype(x.dtype)
    sizes = [Q_WIDTH, KV_WIDTH, KV_WIDTH, D_CONV, D_CONV, D_CONV, D_MODEL, D_MODEL]
    split_at = [int(v) for v in np.cumsum(sizes)[:-1]]

    for l in range(DEPTH):
        h = rmsnorm(x, g_mix[l])
        p = h @ w_in[l]
        q, k, v, cb, cc, cx, ga, gc = jnp.split(p, split_at, axis=-1)
        q = partial_rope(q.reshape(b, s, N_HEADS, HEAD_DIM), cos, sin)
        k = partial_rope(k.reshape(b, s, N_KV_HEADS, HEAD_DIM), cos, sin)
        v = v.reshape(b, s, N_KV_HEADS, HEAD_DIM)
        a = sliding_window_attention(q, k, v, sinks[l]) @ w_attn_out[l]
        c = (cb * causal_short_conv(cc * cx, conv_w[l])) @ w_conv_out[l]
        merged = jax.nn.sigmoid(ga) * a + jax.nn.sigmoid(gc) * c
        x = x + merged @ w_o[l]
        h2 = rmsnorm(x, g_ffn[l])
        x = x + hierarchical_moe(h2, w_group[l], b_group[l], w_router[l], b_router[l],
                                 w_gate[l], w_up[l], w_down[l])
    return rmsnorm(x, g_final)
```

```python
import functools
import math

import jax
import jax.numpy as jnp
from jax import lax
from jax.experimental import pallas as pl
from jax.experimental.pallas import tpu as pltpu

D_MODEL = 2048
N_HEADS = 16
N_KV_HEADS = 4
HEAD_DIM = 64
ROT_DIM = HEAD_DIM // 4
ROPE_THETA = 500000.0
WINDOW = 128
Q_WIDTH = N_HEADS * HEAD_DIM
KV_WIDTH = N_KV_HEADS * HEAD_DIM
D_CONV = D_MODEL // 2
CONV_K = 3
IN_COLS = Q_WIDTH + 2 * KV_WIDTH + 3 * D_CONV + 2 * D_MODEL
N_GROUPS = 4
EXPERTS_PER_GROUP = 8
N_EXPERTS = N_GROUPS * EXPERTS_PER_GROUP
D_FF = 512
NORM_EPS = 1e-5

LANES = 128
BF16_SUBLANES = 16
VMEM_LIMIT = 56 * 1024 * 1024

IP_TM = 1024
IP_TN = 512
_P_PERM = (9, 10, 11, 12, 13, 14, 15, 16, 0, 1, 3, 4, 5, 6, 7, 8, 2)
P_GA, P_GC, P_Q, P_CB, P_CC, P_CX, P_K, P_V = 0, 2048, 4096, 5120, 6144, 7168, 8192, 8448

ATT_BLK = 128
MIX_TM = 256
MOE_TM = 256
ROUTER_ROWS = 40
CMB_TM = 512


def _cparams(sem):
    return pltpu.CompilerParams(dimension_semantics=sem, vmem_limit_bytes=VMEM_LIMIT)


def _inproj_kernel(perm_ref, x_ref, g_ref, w_ref, o_ref, h_ref):
    del perm_ref

    @pl.when(pl.program_id(1) == 0)
    def _():
        x = x_ref[...]
        ms = jnp.mean(x * x, axis=-1, keepdims=True)
        h_ref[...] = (x * lax.rsqrt(ms + NORM_EPS) * g_ref[...]).astype(jnp.bfloat16)

    w = w_ref[...].astype(jnp.bfloat16)
    o_ref[...] = jnp.dot(h_ref[...], w, preferred_element_type=jnp.float32).astype(jnp.bfloat16)


def _inproj(x, g_mix, w_in, layer):
    n = x.shape[0]
    perm = jnp.asarray(_P_PERM, jnp.int32)
    grid_spec = pltpu.PrefetchScalarGridSpec(
        num_scalar_prefetch=1,
        grid=(n // IP_TM, IN_COLS // IP_TN),
        in_specs=[
            pl.BlockSpec((IP_TM, D_MODEL), lambda i, j, perm: (i, 0)),
            pl.BlockSpec((None, 1, D_MODEL), lambda i, j, perm: (layer, 0, 0)),
            pl.BlockSpec((None, D_MODEL, IP_TN), lambda i, j, perm: (layer, 0, perm[j])),
        ],
        out_specs=pl.BlockSpec((IP_TM, IP_TN), lambda i, j, perm: (i, j)),
        scratch_shapes=[pltpu.VMEM((IP_TM, D_MODEL), jnp.bfloat16)],
    )
    return pl.pallas_call(
        _inproj_kernel,
        grid_spec=grid_spec,
        out_shape=jax.ShapeDtypeStruct((n, IN_COLS), jnp.bfloat16),
        compiler_params=_cparams(("arbitrary", "arbitrary")),
        name="inproj",
    )(perm, x, g_mix.reshape(g_mix.shape[0], 1, D_MODEL), w_in)


def _rope_tables(seq):
    pos = jnp.arange(seq, dtype=jnp.float32)
    inv_freq = ROPE_THETA ** (-jnp.arange(0, ROT_DIM, 2, dtype=jnp.float32) / ROT_DIM)
    ang = pos[:, None] * inv_freq[None, :]
    cos = jnp.cos(ang)
    sin = jnp.sin(ang)
    half = ROT_DIM // 2
    rest = HEAD_DIM - ROT_DIM
    ones = jnp.ones((seq, rest), jnp.float32)
    zeros = jnp.zeros((seq, rest), jnp.float32)
    zhalf = jnp.zeros((seq, half), jnp.float32)
    rep = LANES // HEAD_DIM
    cos_t = jnp.tile(jnp.concatenate([cos, cos, ones], axis=1), (1, rep))
    sin_a = jnp.tile(jnp.concatenate([-sin, zhalf, zeros], axis=1), (1, rep))
    sin_b = jnp.tile(jnp.concatenate([zhalf, sin, zeros], axis=1), (1, rep))
    return jnp.stack([cos_t, sin_a, sin_b], axis=0)


def _rope(t, tab):
    half = ROT_DIM // 2
    return (t * tab[0]
            + pltpu.roll(t, LANES - half, axis=1) * tab[1]
            + pltpu.roll(t, half, axis=1) * tab[2])


def _attn_kernel(sink_ref, q_ref, ko_ref, kp_ref, vo_ref, vp_ref, to_ref, tp_ref, o_ref, *, layer,
                 blocks_per_seq):
    n = pl.program_id(0)
    first = (n % blocks_per_seq) == 0
    tab_o = to_ref[...]
    tab_p = tp_ref[...]

    row = lax.broadcasted_iota(jnp.int32, (ATT_BLK, 2 * ATT_BLK), 0)
    col = lax.broadcasted_iota(jnp.int32, (ATT_BLK, 2 * ATT_BLK), 1)
    min_col = jnp.where(first, ATT_BLK, 0)
    mask = (col > row) & (col <= row + WINDOW) & (col >= min_col)
    lane = lax.broadcasted_iota(jnp.int32, (2 * ATT_BLK, LANES), 1)
    lo = lane < HEAD_DIM
    lo_q = lax.broadcasted_iota(jnp.int32, (ATT_BLK, LANES), 1) < HEAD_DIM

    scale = 1.0 / math.sqrt(HEAD_DIM)
    kv_chunks = []
    for c in range(KV_WIDTH // LANES):
        cs = slice(c * LANES, (c + 1) * LANES)
        kc = jnp.concatenate(
            [_rope(kp_ref[:, cs].astype(jnp.float32), tab_p),
             _rope(ko_ref[:, cs].astype(jnp.float32), tab_o)], axis=0)
        vc = jnp.concatenate([vp_ref[:, cs], vo_ref[:, cs]], axis=0).astype(jnp.float32)
        kv_chunks.append((kc, vc, pltpu.roll(kc, HEAD_DIM, axis=1), pltpu.roll(vc, HEAD_DIM, axis=1)))

    for g in range(N_KV_HEADS):
        kc, vc, kr, vr = kv_chunks[g // 2]
        if g % 2 == 0:
            k_lo, k_hi = jnp.where(lo, kc, 0.0), jnp.where(lo, 0.0, kr)
            v_lo, v_hi = jnp.where(lo, vc, 0.0), jnp.where(lo, 0.0, vr)
        else:
            k_lo, k_hi = jnp.where(lo, kr, 0.0), jnp.where(lo, 0.0, kc)
            v_lo, v_hi = jnp.where(lo, vr, 0.0), jnp.where(lo, 0.0, vc)
        kk = jnp.concatenate([k_lo, k_hi], axis=0).astype(jnp.bfloat16)
        vv = jnp.concatenate([v_lo, v_hi], axis=0).astype(jnp.bfloat16)
        for p in (2 * g, 2 * g + 1):
            ps = slice(p * LANES, (p + 1) * LANES)
            qp = (_rope(q_ref[:, ps].astype(jnp.float32), tab_o) * scale).astype(jnp.bfloat16)
            s = lax.dot_general(qp, kk, (((1,), (1,)), ((), ())),
                                preferred_element_type=jnp.float32)
            es, rden = [], []
            for hh in range(2):
                sink = sink_ref[layer, 2 * p + hh]
                sm = jnp.where(mask, s[:, hh * 2 * ATT_BLK:(hh + 1) * 2 * ATT_BLK], -1e30)
                m = jnp.maximum(jnp.max(sm, axis=1, keepdims=True), sink)
                e = jnp.exp(sm - m)
                den = jnp.sum(e, axis=1, keepdims=True) + jnp.exp(sink - m)
                es.append(e.astype(jnp.bfloat16))
                rden.append(1.0 / den)
            o = jnp.dot(jnp.concatenate(es, axis=1), vv, preferred_element_type=jnp.float32)
            o = o * jnp.where(lo_q, rden[0], rden[1])
            o_ref[:, ps] = o.astype(jnp.bfloat16)


def _attention(p, sinks, tabs, layer, seq):
    n = p.shape[0]
    bps = seq // ATT_BLK
    kblk = P_K // KV_WIDTH
    vblk = P_V // KV_WIDTH
    prev = lambda i: jnp.maximum(i - 1, 0)
    return pl.pallas_call(
        functools.partial(_attn_kernel, layer=layer, blocks_per_seq=bps),
        grid=(n // ATT_BLK,),
        in_specs=[
            pl.BlockSpec(memory_space=pltpu.SMEM),
            pl.BlockSpec((ATT_BLK, Q_WIDTH), lambda i: (i, P_Q // Q_WIDTH)),
            pl.BlockSpec((ATT_BLK, KV_WIDTH), lambda i: (i, kblk)),
            pl.BlockSpec((ATT_BLK, KV_WIDTH), lambda i: (prev(i), kblk)),
            pl.BlockSpec((ATT_BLK, KV_WIDTH), lambda i: (i, vblk)),
            pl.BlockSpec((ATT_BLK, KV_WIDTH), lambda i: (prev(i), vblk)),
            pl.BlockSpec((3, ATT_BLK, LANES), lambda i: (0, i % bps, 0)),
            pl.BlockSpec((3, ATT_BLK, LANES), lambda i: (0, prev(i % bps), 0)),
        ],
        out_specs=pl.BlockSpec((ATT_BLK, Q_WIDTH), lambda i: (i, 0)),
        out_shape=jax.ShapeDtypeStruct((n, Q_WIDTH), jnp.bfloat16),
        compiler_params=_cparams(("arbitrary",)),
        name="attn",
    )(sinks, p, p, p, p, p, tabs, tabs)


def _sigmoid(x):
    return 1.0 / (1.0 + jnp.exp(-x))


def _first_argmax(vals, axis_size):
    idx = lax.broadcasted_iota(jnp.int32, vals.shape, 0)
    m = jnp.max(vals, axis=0, keepdims=True)
    first = jnp.min(jnp.where(vals == m, idx, axis_size), axis=0, keepdims=True)
    return m, first


def _mix_kernel(att_ref, ga_ref, gc_ref, cb_ref, cc_ref, cx_ref, cch_ref, cxh_ref, x_ref,
                wa_ref, wc_ref, wo_ref, cw_ref, gf_ref, wr_ref, br_ref,
                xmid_ref, h2_ref, eid_ref, ew_ref, *, tiles_per_seq):
    i = pl.program_id(0)
    seq_start = (i % tiles_per_seq) == 0
    f32 = jnp.float32

    u = cc_ref[...].astype(f32) * cx_ref[...].astype(f32)
    uh = cch_ref[...].astype(f32) * cxh_ref[...].astype(f32)
    uh = jnp.where(seq_start, 0.0, uh)
    h1 = uh[BF16_SUBLANES - 1:BF16_SUBLANES, :]
    h2p = uh[BF16_SUBLANES - 2:BF16_SUBLANES - 1, :]
    row = lax.broadcasted_iota(jnp.int32, u.shape, 0)
    u1 = jnp.where(row == 0, h1, pltpu.roll(u, 1, axis=0))
    u2 = jnp.where(row == 0, h2p, jnp.where(row == 1, h1, pltpu.roll(u, 2, axis=0)))
    cw = cw_ref[...]
    conv = cw[0:1, :] * u2 + cw[1:2, :] * u1 + cw[2:3, :] * u
    cin = (cb_ref[...].astype(f32) * conv).astype(jnp.bfloat16)

    a = jnp.dot(att_ref[...], wa_ref[...], preferred_element_type=f32)
    c = jnp.dot(cin, wc_ref[...], preferred_element_type=f32)
    merged = _sigmoid(ga_ref[...].astype(f32)) * a + _sigmoid(gc_ref[...].astype(f32)) * c
    xm = x_ref[...] + jnp.dot(merged.astype(jnp.bfloat16), wo_ref[...], preferred_element_type=f32)
    xmid_ref[...] = xm

    ms = jnp.mean(xm * xm, axis=-1, keepdims=True)
    h2 = xm * lax.rsqrt(ms + NORM_EPS) * gf_ref[...]
    h2_ref[...] = h2

    logits = lax.dot_general(wr_ref[...], h2, (((1,), (1,)), ((), ())),
                             precision=lax.Precision.HIGHEST,
                             preferred_element_type=f32) + br_ref[...]
    gl = logits[0:N_GROUPS, :]
    gmax, gidx = _first_argmax(gl, N_GROUPS)
    gw = 1.0 / jnp.sum(jnp.exp(gl - gmax), axis=0, keepdims=True)
    sel = jnp.zeros((EXPERTS_PER_GROUP, gl.shape[1]), f32)
    for g in range(N_GROUPS):
        lo_r = N_GROUPS + g * EXPERTS_PER_GROUP
        sel = jnp.where(gidx == g, logits[lo_r:lo_r + EXPERTS_PER_GROUP, :], sel)
    m1, i1 = _first_argmax(sel, EXPERTS_PER_GROUP)
    eidx = lax.broadcasted_iota(jnp.int32, sel.shape, 0)
    m2, i2 = _first_argmax(jnp.where(eidx == i1, -jnp.inf, sel), EXPERTS_PER_GROUP)
    p2 = jnp.exp(m2 - m1)
    den = 1.0 + p2
    w1 = gw * (1.0 / den)
    w2 = gw * (p2 / den)
    eid_ref[...] = jnp.concatenate([gidx * EXPERTS_PER_GROUP + i1, gidx * EXPERTS_PER_GROUP + i2], axis=0)
    ew_ref[...] = jnp.concatenate([w1, w2], axis=0)


def _mix(att, p, x, wa, wc, wo, conv_w, g_ffn, wr_t, br, layer, seq):
    n = x.shape[0]
    tps = seq // MIX_TM
    hb = MIX_TM // BF16_SUBLANES
    const = dict(pipeline_mode=pl.Buffered(1))
    hprev = lambda i: jnp.maximum(i * hb - 1, 0)
    return pl.pallas_call(
        functools.partial(_mix_kernel, tiles_per_seq=tps),
        grid=(n // MIX_TM,),
        in_specs=[
            pl.BlockSpec((MIX_TM, Q_WIDTH), lambda i: (i, 0)),
            pl.BlockSpec((MIX_TM, D_MODEL), lambda i: (i, P_GA // D_MODEL)),
            pl.BlockSpec((MIX_TM, D_MODEL), lambda i: (i, P_GC // D_MODEL)),
            pl.BlockSpec((MIX_TM, D_CONV), lambda i: (i, P_CB // D_CONV)),
            pl.BlockSpec((MIX_TM, D_CONV), lambda i: (i, P_CC // D_CONV)),
            pl.BlockSpec((MIX_TM, D_CONV), lambda i: (i, P_CX // D_CONV)),
            pl.BlockSpec((BF16_SUBLANES, D_CONV), lambda i: (hprev(i), P_CC // D_CONV)),
            pl.BlockSpec((BF16_SUBLANES, D_CONV), lambda i: (hprev(i), P_CX // D_CONV)),
            pl.BlockSpec((MIX_TM, D_MODEL), lambda i: (i, 0)),
            pl.BlockSpec((None, Q_WIDTH, D_MODEL), lambda i: (layer, 0, 0), **const),
            pl.BlockSpec((None, D_CONV, D_MODEL), lambda i: (layer, 0, 0), **const),
            pl.BlockSpec((None, D_MODEL, D_MODEL), lambda i: (layer, 0, 0), **const),
            pl.BlockSpec((None, CONV_K, D_CONV), lambda i: (layer, 0, 0)),
            pl.BlockSpec((None, 1, D_MODEL), lambda i: (layer, 0, 0)),
            pl.BlockSpec((None, ROUTER_ROWS, D_MODEL), lambda i: (layer, 0, 0)),
            pl.BlockSpec((None, ROUTER_ROWS, 1), lambda i: (layer, 0, 0)),
        ],
        out_specs=[
            pl.BlockSpec((MIX_TM, D_MODEL), lambda i: (i, 0)),
            pl.BlockSpec((MIX_TM, D_MODEL), lambda i: (i, 0)),
            pl.BlockSpec((2, MIX_TM), lambda i: (0, i)),
            pl.BlockSpec((2, MIX_TM), lambda i: (0, i)),
        ],
        out_shape=[
            jax.ShapeDtypeStruct((n, D_MODEL), jnp.float32),
            jax.ShapeDtypeStruct((n, D_MODEL), jnp.float32),
            jax.ShapeDtypeStruct((2, n), jnp.int32),
            jax.ShapeDtypeStruct((2, n), jnp.float32),
        ],
        compiler_params=_cparams(("arbitrary",)),
        name="mix",
    )(att, p, p, p, p, p, p, p, x, wa, wc, wo, conv_w,
      g_ffn.reshape(g_ffn.shape[0], 1, D_MODEL), wr_t, br)


def _moe_max_tiles(n_rows):
    return n_rows // MOE_TM + N_EXPERTS


def _route_plan(eid, ew):
    n = eid.shape[1]
    n_rows = 2 * n
    t_max = _moe_max_tiles(n_rows)
    e_flat = eid.reshape(-1)
    w_flat = ew.reshape(-1)
    order = jnp.argsort(e_flat, stable=True).astype(jnp.int32)
    e_sorted = e_flat[order]
    starts = jnp.searchsorted(e_sorted, jnp.arange(N_EXPERTS + 1, dtype=jnp.int32), side="left").astype(jnp.int32)
    counts = starts[1:] - starts[:-1]
    padded = ((counts + MOE_TM - 1) // MOE_TM) * MOE_TM
    pstart = jnp.concatenate([jnp.zeros((1,), jnp.int32), jnp.cumsum(padded).astype(jnp.int32)])
    n_active = pstart[N_EXPERTS] // MOE_TM
    tile_row0 = jnp.arange(t_max, dtype=jnp.int32) * MOE_TM
    tile_e = jnp.searchsorted(pstart[1:], tile_row0, side="right").astype(jnp.int32)
    last_e = tile_e[jnp.maximum(n_active - 1, 0)]
    tile_e = jnp.where(jnp.arange(t_max) < n_active, tile_e, last_e)
    r = jnp.arange(t_max * MOE_TM, dtype=jnp.int32)
    e_r = tile_e[r // MOE_TM]
    j = r - pstart[e_r] + starts[e_r]
    valid = (j < starts[e_r + 1]) & (r < pstart[N_EXPERTS])
    src = order[jnp.clip(j, 0, n_rows - 1)]
    row_tok = jnp.where(valid, src % n, 0)
    dump = n_rows + ((r // MOE_TM) % 2) * MOE_TM + (r % MOE_TM)
    row_dst = jnp.where(valid, src, dump)
    row_w = jnp.where(valid, w_flat[src], 0.0)
    return (tile_e, n_active.reshape(1), row_tok.reshape(t_max, 1, MOE_TM),
            row_dst.reshape(t_max, 1, MOE_TM), row_w.reshape(t_max * MOE_TM, 1))


def _moe_kernel(te_ref, na_ref, tok_ref, dst_ref, rw_ref, h2_hbm, wg_ref, wu_ref, wd_ref, y_hbm,
                xbuf, ybuf, wgu_bf, wd_bf, sem_g, sem_s):
    t = pl.program_id(0)

    def row_gather(r):
        return pltpu.make_async_copy(h2_hbm.at[pl.ds(tok_ref[0, 0, r], 1), :],
                                     xbuf.at[pl.ds(r, 1), :], sem_g)

    def row_scatter(r):
        return pltpu.make_async_copy(ybuf.at[pl.ds(r, 1), :],
                                     y_hbm.at[pl.ds(dst_ref[0, 0, r], 1), :], sem_s)

    @pl.when(t == 0)
    def _():
        n_real = y_hbm.shape[0] - 2 * MOE_TM
        ybuf[...] = jnp.zeros_like(ybuf)
        for half in range(2):
            cp = pltpu.make_async_copy(ybuf, y_hbm.at[pl.ds(n_real + half * MOE_TM, MOE_TM), :], sem_s)
            cp.start()
            cp.wait()

    @pl.when(t < na_ref[0])
    def _():
        def issue_g(r, carry):
            row_gather(r).start()
            return carry
        lax.fori_loop(0, MOE_TM, issue_g, 0, unroll=8)

        e = te_ref[t]
        e_prev = te_ref[jnp.maximum(t - 1, 0)]

        @pl.when((t == 0) | (e != e_prev))
        def _():
            wgu_bf[:, :D_FF] = wg_ref[...].astype(jnp.bfloat16)
            wgu_bf[:, D_FF:] = wu_ref[...].astype(jnp.bfloat16)
            wd_bf[...] = wd_ref[...].astype(jnp.bfloat16)

        pltpu.make_async_copy(h2_hbm.at[pl.ds(0, MOE_TM), :], xbuf, sem_g).wait()

        xb = xbuf[...].astype(jnp.bfloat16)
        gu = jnp.dot(xb, wgu_bf[...], preferred_element_type=jnp.float32)
        gate = gu[:, :D_FF]
        act = gate * _sigmoid(gate) * gu[:, D_FF:] * rw_ref[...]
        ybuf[...] = jnp.dot(act.astype(jnp.bfloat16), wd_bf[...], preferred_element_type=jnp.float32)

        def issue_s(r, carry):
            row_scatter(r).start()
            return carry
        lax.fori_loop(0, MOE_TM, issue_s, 0, unroll=8)
        pltpu.make_async_copy(ybuf, y_hbm.at[pl.ds(0, MOE_TM), :], sem_s).wait()


def _moe(h2, plan, w_gate, w_up, w_down, layer):
    tile_e, n_active, row_tok, row_dst, row_w = plan
    n = h2.shape[0]
    t_max = tile_e.shape[0]
    smem_rows = lambda: pl.BlockSpec((1, 1, MOE_TM), lambda t, te, na: (t, 0, 0), memory_space=pltpu.SMEM)
    grid_spec = pltpu.PrefetchScalarGridSpec(
        num_scalar_prefetch=2,
        grid=(t_max,),
        in_specs=[
            smem_rows(),
            smem_rows(),
            pl.BlockSpec((MOE_TM, 1), lambda t, te, na: (t, 0)),
            pl.BlockSpec(memory_space=pl.ANY),
            pl.BlockSpec((None, None, D_MODEL, D_FF), lambda t, te, na: (layer, te[t], 0, 0)),
            pl.BlockSpec((None, None, D_MODEL, D_FF), lambda t, te, na: (layer, te[t], 0, 0)),
            pl.BlockSpec((None, None, D_FF, D_MODEL), lambda t, te, na: (layer, te[t], 0, 0)),
        ],
        out_specs=pl.BlockSpec(memory_space=pl.ANY),
        scratch_shapes=[
            pltpu.VMEM((MOE_TM, D_MODEL), jnp.float32),
            pltpu.VMEM((MOE_TM, D_MODEL), jnp.float32),
            pltpu.VMEM((D_MODEL, 2 * D_FF), jnp.bfloat16),
            pltpu.VMEM((D_FF, D_MODEL), jnp.bfloat16),
            pltpu.SemaphoreType.DMA(()),
            pltpu.SemaphoreType.DMA(()),
        ],
    )
    return pl.pallas_call(
        _moe_kernel,
        grid_spec=grid_spec,
        out_shape=jax.ShapeDtypeStruct((2 * n + 2 * MOE_TM, D_MODEL), jnp.float32),
        compiler_params=_cparams(("arbitrary",)),
        name="moe",
    )(tile_e, n_active, row_tok, row_dst, row_w, h2, w_gate, w_up, w_down)


def _combine_kernel(x_ref, y0_ref, y1_ref, o_ref):
    o_ref[...] = x_ref[...] + y0_ref[...] + y1_ref[...]


def _combine_norm_kernel(x_ref, y0_ref, y1_ref, g_ref, o_ref):
    x = x_ref[...] + y0_ref[...] + y1_ref[...]
    ms = jnp.mean(x * x, axis=-1, keepdims=True)
    o_ref[...] = x * lax.rsqrt(ms + NORM_EPS) * g_ref[...]


def _combine(xmid, y, g_final=None):
    n = xmid.shape[0]
    nt = n // CMB_TM
    in_specs = [
        pl.BlockSpec((CMB_TM, D_MODEL), lambda i: (i, 0)),
        pl.BlockSpec((CMB_TM, D_MODEL), lambda i: (i, 0)),
        pl.BlockSpec((CMB_TM, D_MODEL), lambda i: (i + nt, 0)),
    ]
    args = [xmid, y, y]
    kern = _combine_kernel
    if g_final is not None:
        in_specs.append(pl.BlockSpec((1, D_MODEL), lambda i: (0, 0)))
        args.append(g_final.reshape(1, D_MODEL))
        kern = _combine_norm_kernel
    return pl.pallas_call(
        kern,
        grid=(nt,),
        in_specs=in_specs,
        out_specs=pl.BlockSpec((CMB_TM, D_MODEL), lambda i: (i, 0)),
        out_shape=jax.ShapeDtypeStruct((n, D_MODEL), jnp.float32),
        compiler_params=_cparams(("arbitrary",)),
        name="combine",
    )(*args)


def kernel(x, g_mix, w_in, sinks, w_attn_out, conv_w, w_conv_out, w_o, g_ffn, w_group, b_group,
           w_router, b_router, w_gate, w_up, w_down, g_final):
    b, s, d = x.shape
    depth = w_in.shape[0]
    n = b * s
    assert d == D_MODEL and s % MIX_TM == 0 and s % ATT_BLK == 0 and n % IP_TM == 0 and n % CMB_TM == 0
    xf = x.reshape(n, d)
    tabs = _rope_tables(s)
    wa = w_attn_out.astype(jnp.bfloat16)
    wc = w_conv_out.astype(jnp.bfloat16)
    wo = w_o.astype(jnp.bfloat16)
    pad_r = ROUTER_ROWS - N_GROUPS - N_EXPERTS
    wr_t = jnp.pad(jnp.concatenate([w_group, w_router], axis=-1).transpose(0, 2, 1),
                   ((0, 0), (0, pad_r), (0, 0)))
    br = jnp.pad(jnp.concatenate([b_group, b_router], axis=-1), ((0, 0), (0, pad_r)))[..., None]

    for layer in range(depth):
        p = _inproj(xf, g_mix, w_in, layer)
        att = _attention(p, sinks, tabs, layer, s)
        xmid, h2, eid, ew = _mix(att, p, xf, wa, wc, wo, conv_w, g_ffn, wr_t, br, layer, s)
        plan = _route_plan(eid, ew)
        y = _moe(h2, plan, w_gate, w_up, w_down, layer)
        xf = _combine(xmid, y, g_final if layer == depth - 1 else None)
    return xf.reshape(b, s, d)
```
